```python
import jax
import jax.numpy as jnp
from jax import lax
import numpy as np

D_MODEL = 2048
BATCH = 2
SEQ = 4096
DEPTH = 1
DEC_BATCH = 8
DEC_SEQ = 8
PAST_LEN = 16384
PAGE_SIZE = 128

N_HEADS = 16
HEAD_DIM = 64
N_KV_HEADS = 4
ATT_WIDTH = N_HEADS * HEAD_DIM
KV_WIDTH = N_KV_HEADS * HEAD_DIM
IDX_HEADS = 16
IDX_DIM = 64
TOPK_MAX = 256
Q_BLOCK = 128
POOL_WINDOWS = (2, 4, 8, 16)
POOL_GROUPS = 4
POOL_WIDTH = D_MODEL // 2
POOL_GROUP_WIDTH = POOL_WIDTH // POOL_GROUPS
POOL_STATE = max(POOL_WINDOWS) - 1
N_EXPERTS = 32
TOP_K = 4
D_FF = D_MODEL
SWIGLU_LIMIT = 7.0
SWIGLU_ALPHA = 1.702
MOE_BLOCK = 128
ALPHA = (2 * DEPTH) ** 0.25
BETA = (8 * DEPTH) ** -0.25
LN_EPS = 1e-5
IN_SPLITS = (ATT_WIDTH, KV_WIDTH, KV_WIDTH, IDX_HEADS * IDX_DIM, IDX_DIM, IDX_HEADS, POOL_WIDTH, D_MODEL, D_MODEL)
IN_WIDTH = sum(IN_SPLITS)

kernel_name = "dsa_pool_parallel_moe_step"


def _split_points():
    pts, acc = [], 0
    for w in IN_SPLITS[:-1]:
        acc += w
        pts.append(acc)
    return pts


def alibi_slopes():
    return jnp.exp2(-8.0 * jnp.arange(1, N_HEADS + 1, dtype=jnp.float32) / N_HEADS)


def layer_norm(x, g, b):
    xf = x.astype(jnp.float32)
    mu = xf.mean(-1, keepdims=True)
    var = jnp.square(xf - mu).mean(-1, keepdims=True)
    return ((xf - mu) * lax.rsqrt(var + LN_EPS) * g.astype(jnp.float32) + b.astype(jnp.float32)).astype(x.dtype)


def in_proj(x, w_in):
    B, S, _ = x.shape
    h = jnp.einsum('bsd,de->bse', x, w_in)
    q, k, v, qi, ki, wi, u, ga, gb = jnp.split(h, _split_points(), axis=-1)
    return (q.reshape(B, S, N_HEADS, HEAD_DIM), k.reshape(B, S, N_KV_HEADS, HEAD_DIM),
            v.reshape(B, S, N_KV_HEADS, HEAD_DIM), qi.reshape(B, S, IDX_HEADS, IDX_DIM),
            ki, wi, u, ga, gb)


def indexer_topk(qi, wi, ki, pos_q, k_top):
    dots = jnp.einsum('bqhd,bld->bqhl', qi, ki).astype(jnp.float32) * (IDX_DIM ** -0.5)
    score = jnp.einsum('bqhl,bqh->bql', jax.nn.relu(dots), wi.astype(jnp.float32)) * (IDX_HEADS ** -0.5)
    pos_k = jnp.arange(ki.shape[1])
    score = jnp.where(pos_k[None, None, :] <= pos_q[None, :, None], score, -jnp.inf)
    _, idx = lax.top_k(score, k_top)
    return idx


def sparse_attend(q, k_sel, v_sel, idx, pos_q, slopes):
    B, Q = q.shape[:2]
    R = N_HEADS // N_KV_HEADS
    qg = q.reshape(B, Q, N_KV_HEADS, R, HEAD_DIM)
    s = jnp.einsum('bqgrd,bqkgd->bqgrk', qg, k_sel).astype(jnp.float32) * (HEAD_DIM ** -0.5)
    dist = (pos_q[None, :, None] - idx).astype(jnp.float32)
    s = s - slopes.reshape(N_KV_HEADS, R)[None, None, :, :, None] * dist[:, :, None, None, :]
    s = jnp.where((dist >= 0)[:, :, None, None, :], s, -jnp.inf)
    p = jax.nn.softmax(s, axis=-1).astype(v_sel.dtype)
    o = jnp.einsum('bqgrk,bqkgd->bqgrd', p, v_sel)
    return o.reshape(B, Q, ATT_WIDTH)


def _gather_rows(a, i):
    return jax.vmap(lambda ab, ib: ab[ib])(a, i)


def prompt_attention(q, k, v, qi, wi, ki, slopes):
    B, S = q.shape[:2]
    k_top = min(TOPK_MAX, S // 4)
    nb = S // Q_BLOCK

    def blockify(a):
        return jnp.moveaxis(a.reshape((B, nb, Q_BLOCK) + a.shape[2:]), 1, 0)

    def one_block(args):
        qb, qib, wib, start = args
        pos_q = start + jnp.arange(Q_BLOCK)
        idx = indexer_topk(qib, wib, ki, pos_q, k_top)
        return sparse_attend(qb, _gather_rows(k, idx), _gather_rows(v, idx), idx, pos_q, slopes)

    starts = jnp.arange(nb) * Q_BLOCK
    out = lax.map(one_block, (blockify(q), blockify(qi), blockify(wi), starts))
    return jnp.moveaxis(out, 0, 1).reshape(B, S, ATT_WIDTH)


def sample_attention(q, k_new, v_new, qi, wi, ki_new, cache_k, cache_v, cache_ki, page_table, slopes):
    DB, DS = q.shape[:2]
    n_pages = page_table.shape[1]
    page = cache_k.shape[1]
    past = n_pages * page
    k_top = min(TOPK_MAX, (past + DS) // 4)
    ki_past = cache_ki[page_table].reshape(DB, past, IDX_DIM)
    ki_all = jnp.concatenate([ki_past, ki_new.astype(ki_past.dtype)], axis=1)
    pos_q = past + jnp.arange(DS)
    idx = indexer_topk(qi, wi, ki_all, pos_q, k_top)
    in_past = idx < past
    pidx = jnp.minimum(idx, past - 1)
    phys = jnp.take_along_axis(page_table, (pidx // page).reshape(DB, -1), axis=1).reshape(idx.shape)
    off = pidx % page
    nidx = jnp.clip(idx - past, 0, DS - 1)

    def select(cache, new):
        return jnp.where(in_past[..., None, None], cache[phys, off], _gather_rows(new.astype(cache.dtype), nidx))

    return sparse_attend(q, select(cache_k, k_new), select(cache_v, v_new), idx, pos_q, slopes)


def pool_mix(u_prev, u, pos0, pool_w, pool_scale):
    B, S, _ = u.shape
    P = POOL_STATE
    ext = jnp.concatenate([u_prev.astype(u.dtype), u], axis=1)
    c = jnp.cumsum(ext.astype(jnp.float32), axis=1)
    c = jnp.concatenate([jnp.zeros_like(c[:, :1]), c], axis=1)
    pos = pos0 + jnp.arange(S)
    outs = []
    for g, w in enumerate(POOL_WINDOWS):
        sl = slice(g * POOL_GROUP_WIDTH, (g + 1) * POOL_GROUP_WIDTH)
        win_sum = c[:, P + 1:P + 1 + S, sl] - c[:, P + 1 - w:P + 1 - w + S, sl]
        cnt = jnp.minimum(w, pos + 1).astype(jnp.float32)
        outs.append(win_sum / cnt[None, :, None])
    pooled = jnp.stack(outs, axis=2)
    diff = (pooled - u.reshape(B, S, POOL_GROUPS, POOL_GROUP_WIDTH).astype(jnp.float32)).astype(u.dtype)
    mixed = jnp.einsum('bsgc,gcd->bsgd', diff, pool_w).reshape(B, S, POOL_WIDTH)
    return mixed * pool_scale, ext[:, -P:]


def mixer_merge(att, pool, ga, gb, w_oa, w_ob, w_out):
    m = (jax.nn.sigmoid(ga) * jnp.einsum('bse,ed->bsd', att, w_oa)
         + jax.nn.sigmoid(gb) * jnp.einsum('bse,ed->bsd', pool, w_ob))
    return jnp.einsum('bsd,de->bse', m, w_out)


def moe_ffn(xt, w_router, b_router, w_gu, b_gu, w_down, b_down):
    T, D = xt.shape
    A = T * TOP_K
    nb = A // MOE_BLOCK + N_EXPERTS
    logits = jnp.einsum('td,de->te', xt, w_router).astype(jnp.float32) + b_router.astype(jnp.float32)
    top_v, top_e = lax.top_k(logits, TOP_K)
    gates = jax.nn.softmax(top_v, axis=-1).reshape(A)
    flat_e = top_e.reshape(A)
    order = jnp.argsort(flat_e)
    e_sorted = flat_e[order]
    tok_sorted = (order // TOP_K).astype(jnp.int32)
    sizes = jnp.bincount(flat_e, length=N_EXPERTS)
    padded = (sizes + MOE_BLOCK - 1) // MOE_BLOCK * MOE_BLOCK
    ends = jnp.cumsum(padded)
    dest = (ends - padded)[e_sorted] + jnp.arange(A) - (jnp.cumsum(sizes) - sizes)[e_sorted]
    tok_rows = jnp.zeros((nb * MOE_BLOCK,), jnp.int32).at[dest].set(tok_sorted)
    block_e = jnp.minimum(jnp.searchsorted(ends, jnp.arange(nb) * MOE_BLOCK, side='right'), N_EXPERTS - 1)

    def expert_block(args):
        xb, e = args
        h = xb @ w_gu[e] + b_gu[e]
        g = jnp.minimum(h[:, :D_FF], SWIGLU_LIMIT)
        up = jnp.clip(h[:, D_FF:], -SWIGLU_LIMIT, SWIGLU_LIMIT)
        return ((up + 1) * (g * jax.nn.sigmoid(SWIGLU_ALPHA * g))) @ w_down[e] + b_down[e]

    out_rows = lax.map(expert_block, (xt[tok_rows].reshape(nb, MOE_BLOCK, D), block_e)).reshape(nb * MOE_BLOCK, D)
    contrib = out_rows[dest] * gates[order][:, None].astype(out_rows.dtype)
    return jnp.zeros_like(xt).at[tok_sorted].add(contrib)


def setup_inputs(seed: int = 0) -> dict:
    key = jax.random.key(seed)
    ks = jax.random.split(key, 24)
    f32 = jnp.float32
    n_pages = PAST_LEN // PAGE_SIZE
    n_used = DEC_BATCH * n_pages
    n_phys = n_used + max(1, n_used // 4)

    def nrm(k, shape, scale=1.0):
        return jax.random.normal(k, shape, f32) * scale

    v_lo = ATT_WIDTH + KV_WIDTH
    col_scale = jnp.ones((IN_WIDTH,), f32).at[v_lo:v_lo + KV_WIDTH].set(BETA)
    GW = POOL_GROUP_WIDTH
    return {
        "x_prompt": nrm(ks[0], (BATCH, SEQ, D_MODEL)),
        "x_sample": nrm(ks[1], (DEC_BATCH, DEC_SEQ, D_MODEL)),
        "cache_k": nrm(ks[2], (DEPTH, n_phys, PAGE_SIZE, N_KV_HEADS, HEAD_DIM)),
        "cache_v": nrm(ks[3], (DEPTH, n_phys, PAGE_SIZE, N_KV_HEADS, HEAD_DIM), BETA),
        "cache_kidx": nrm(ks[4], (DEPTH, n_phys, PAGE_SIZE, IDX_DIM)),
        "state_pool": nrm(ks[5], (DEPTH, DEC_BATCH, POOL_STATE, POOL_WIDTH)),
        "page_table": jax.random.permutation(ks[6], n_phys)[:n_used].reshape(DEC_BATCH, n_pages).astype(jnp.int32),
        "w_in": nrm(ks[7], (DEPTH, D_MODEL, IN_WIDTH), D_MODEL ** -0.5) * col_scale,
        "w_oa": nrm(ks[8], (DEPTH, ATT_WIDTH, D_MODEL), ATT_WIDTH ** -0.5),
        "w_ob": nrm(ks[9], (DEPTH, POOL_WIDTH, D_MODEL), POOL_WIDTH ** -0.5),
        "pool_w": nrm(ks[10], (DEPTH, POOL_GROUPS, GW, GW), GW ** -0.5),
        "pool_scale": 1.0 + nrm(ks[11], (DEPTH, POOL_WIDTH), 0.1),
        "w_out": nrm(ks[12], (DEPTH, D_MODEL, D_MODEL), D_MODEL ** -0.5 * BETA),
        "ln1_g": 1.0 + nrm(ks[13], (DEPTH, D_MODEL), 0.02),
        "ln1_b": nrm(ks[14], (DEPTH, D_MODEL), 0.02),
        "w_router": nrm(ks[15], (DEPTH, D_MODEL, N_EXPERTS), D_MODEL ** -0.5),
        "b_router": nrm(ks[16], (DEPTH, N_EXPERTS), 0.01),
        "w_gu": nrm(ks[17], (DEPTH, N_EXPERTS, D_MODEL, 2 * D_FF), D_MODEL ** -0.5 * BETA),
        "b_gu": nrm(ks[18], (DEPTH, N_EXPERTS, 2 * D_FF), 0.01),
        "w_down": nrm(ks[19], (DEPTH, N_EXPERTS, D_FF, D_MODEL), D_FF ** -0.5 * BETA),
        "b_down": nrm(ks[20], (DEPTH, N_EXPERTS, D_MODEL), 0.01),
        "ln2_g": 1.0 + nrm(ks[21], (DEPTH, D_MODEL), 0.02),
        "ln2_b": nrm(ks[22], (DEPTH, D_MODEL), 0.02),
    }


def reference(x_prompt, x_sample, cache_k, cache_v, cache_kidx, state_pool, page_table,
              w_in, w_oa, w_ob, pool_w, pool_scale, w_out, ln1_g, ln1_b,
              w_router, b_router, w_gu, b_gu, w_down, b_down, ln2_g, ln2_b):
    slopes = alibi_slopes()
    past_len = page_table.shape[1] * cache_k.shape[2]
    xp, xs = x_prompt, x_sample
    B, S, D = xp.shape
    DB, DS, _ = xs.shape
    kp_l, vp_l, kip_l, pp_l, ks_l, vs_l, kis_l, ps_l = [], [], [], [], [], [], [], []
    for l in range(DEPTH):
        qp, kp, vp, qip, kip, wip, up, gap, gbp = in_proj(xp, w_in[l])
        att_p = prompt_attention(qp, kp, vp, qip, wip, kip, slopes)
        pool_p, pst_p = pool_mix(jnp.zeros((B, POOL_STATE, POOL_WIDTH), up.dtype), up, 0, pool_w[l], pool_scale[l])
        mix_p = mixer_merge(att_p, pool_p, gap, gbp, w_oa[l], w_ob[l], w_out[l])
        qs, ks_, vs, qis, kis, wis, us, gas, gbs = in_proj(xs, w_in[l])
        att_s = sample_attention(qs, ks_, vs, qis, wis, kis, cache_k[l], cache_v[l], cache_kidx[l], page_table, slopes)
        pool_s, pst_s = pool_mix(state_pool[l], us, past_len, pool_w[l], pool_scale[l])
        mix_s = mixer_merge(att_s, pool_s, gas, gbs, w_oa[l], w_ob[l], w_out[l])
        hp = layer_norm(ALPHA * xp + mix_p, ln1_g[l], ln1_b[l])
        hs = layer_norm(ALPHA * xs + mix_s, ln1_g[l], ln1_b[l])
        h_tok = jnp.concatenate([hp.reshape(B * S, D), hs.reshape(DB * DS, D)], axis=0)
        f_tok = moe_ffn(h_tok, w_router[l], b_router[l], w_gu[l], b_gu[l], w_down[l], b_down[l])
        xp = layer_norm(ALPHA * hp + f_tok[:B * S].reshape(B, S, D), ln2_g[l], ln2_b[l])
        xs = layer_norm(ALPHA * hs + f_tok[B * S:].reshape(DB, DS, D), ln2_g[l], ln2_b[l])
        kp_l.append(kp); vp_l.append(vp); kip_l.append(kip); pp_l.append(pst_p)
        ks_l.append(ks_); vs_l.append(vs); kis_l.append(kis); ps_l.append(pst_s)
    return (xp, xs, jnp.stack(kp_l), jnp.stack(vp_l), jnp.stack(kip_l), jnp.stack(pp_l),
            jnp.stack(ks_l), jnp.stack(vs_l), jnp.stack(kis_l), jnp.stack(ps_l))
```

```python
import functools

import jax
import jax.numpy as jnp
import numpy as np
from jax import lax
from jax.experimental import pallas as pl
from jax.experimental.pallas import tpu as pltpu

F32 = jnp.float32
BF16 = jnp.bfloat16
I32 = jnp.int32
U32 = jnp.uint32

N_HEADS = 16
HEAD_DIM = 64
N_KV_HEADS = 4
IDX_HEADS = 16
IDX_DIM = 64
TOPK_MAX = 256
Q_BLOCK = 128
POOL_WINDOWS = (2, 4, 8, 16)
POOL_STATE = 15
TOP_K = 4
SWIGLU_LIMIT = 7.0
SWIGLU_ALPHA = 1.702
LN_EPS = 1e-5
DEPTH = 1
ALPHA = (2 * DEPTH) ** 0.25

LANES = 128
SUBLANES = 8
VMEM_LIMIT_BYTES = 56 * 1024 * 1024

INT_MIN = -(2 ** 31)
NEG_BIG = -1e30

ATT_W = N_HEADS * HEAD_DIM
KV_W = N_KV_HEADS * HEAD_DIM
IDX_W = IDX_HEADS * IDX_DIM
IN_TN = 512


def _cparams(sem, vmem=VMEM_LIMIT_BYTES):
    return pltpu.CompilerParams(dimension_semantics=sem, vmem_limit_bytes=vmem)


def _inproj_kernel(x_ref, w_ref, o_ref, xb_ref):
    @pl.when(pl.program_id(1) == 0)
    def _():
        xb_ref[...] = x_ref[...].astype(BF16)

    o_ref[...] = jnp.dot(xb_ref[...], w_ref[...], preferred_element_type=F32)


def _in_proj(x2d, w_p, tm):
    t, d = x2d.shape
    n = w_p.shape[1]
    return pl.pallas_call(
        _inproj_kernel,
        grid=(t // tm, n // IN_TN),
        in_specs=[pl.BlockSpec((tm, d), lambda i, j: (i, 0)),
                  pl.BlockSpec((d, IN_TN), lambda i, j: (0, j))],
        out_specs=pl.BlockSpec((tm, IN_TN), lambda i, j: (i, j)),
        out_shape=jax.ShapeDtypeStruct((t, n), F32),
        scratch_shapes=[pltpu.VMEM((tm, d), BF16)],
        compiler_params=_cparams(("parallel", "arbitrary")),
        name="in_proj",
    )(x2d, w_p)


def _kvcast_kernel(kv_ref, kiw_ref, kvp_ref, kd_ref):
    kv = kv_ref[...]
    for c in range(4):
        kvp_ref[c] = kv[:, LANES * c:LANES * (c + 1)].astype(BF16)
    x = kiw_ref[...]
    lane = lax.broadcasted_iota(I32, x.shape, 1)
    kd_ref[...] = jnp.where(lane < IDX_DIM, x, pltpu.roll(x, IDX_DIM, axis=1)).astype(BF16)


def _kv_cast(h, col_kv, col_kiw, tm):
    t = h.shape[0]
    return pl.pallas_call(
        _kvcast_kernel,
        grid=(t // tm,),
        in_specs=[pl.BlockSpec((tm, 2 * KV_W), lambda i: (i, col_kv // (2 * KV_W))),
                  pl.BlockSpec((tm, LANES), lambda i: (i, col_kiw // LANES))],
        out_specs=[pl.BlockSpec((4, tm, LANES), lambda i: (0, i, 0)),
                   pl.BlockSpec((tm, LANES), lambda i: (i, 0))],
        out_shape=[jax.ShapeDtypeStruct((4, t, LANES), BF16),
                   jax.ShapeDtypeStruct((t, LANES), BF16)],
        compiler_params=_cparams(("parallel",)),
        name="kv_cast",
    )(h, h)


KC = 256


def _attn_kernel(q_ref, qi_ref, wiw_ref, kvp_ref, kd_ref, tab_ref, o_ref,
                 lhsq_ref, lhsi_ref, wib_ref, keys_ref, s_ref, oh_ref, tb_ref, *, k_top, n_qb):
    i = pl.program_id(1)
    n_ch = i // 2 + 1
    qb = Q_BLOCK
    lane_sq = lax.broadcasted_iota(I32, (qb, LANES), 1)
    low = lane_sq < HEAD_DIM

    qf = q_ref[...] * (HEAD_DIM ** -0.5)
    for h in range(N_HEADS):
        p, kh = h // 2, (h // 4) % 2
        qp = qf[:, LANES * p:LANES * (p + 1)]
        if (h % 2) != kh:
            qp = pltpu.roll(qp, HEAD_DIM, axis=1)
        lhsq_ref[h] = jnp.where(low if kh == 0 else jnp.logical_not(low), qp, 0.0).astype(BF16)
    qif = qi_ref[...]
    for h in range(IDX_HEADS):
        p = h // 2
        qp = qif[:, LANES * p:LANES * (p + 1)]
        lhsi_ref[h] = jnp.where(low if h % 2 == 0 else jnp.logical_not(low), qp, 0.0).astype(BF16)
    wiw = wiw_ref[...]
    wscale = (IDX_DIM ** -0.5) * (IDX_HEADS ** -0.5)
    for h in range(IDX_HEADS):
        col = wiw[:, IDX_DIM + h:IDX_DIM + h + 1] * wscale
        wib_ref[h] = jnp.broadcast_to(col, (qb, LANES))

    row = lax.broadcasted_iota(I32, (qb, KC), 0)
    lane = lax.broadcasted_iota(I32, (qb, KC), 1)
    qpos = i * qb + row
    nt = (((1,), (1,)), ((), ()))

    def idx_body(c, carry):
        kd = kd_ref[pl.ds(pl.multiple_of(c * KC, KC), KC), :]
        acc = jnp.zeros((qb, KC), F32)
        for h in range(IDX_HEADS):
            d = lax.dot_general(lhsi_ref[h], kd, nt, preferred_element_type=F32)
            w = wib_ref[h]
            acc = acc + jnp.maximum(d, 0.0) * jnp.concatenate([w, w], axis=1)
        bits = pltpu.bitcast(acc, I32)
        key = jnp.where(bits < 0, bits ^ jnp.int32(0x7FFFFFFF), bits)
        keys_ref[c] = jnp.where(c * KC + lane <= qpos, key, jnp.int32(INT_MIN))
        return carry

    lax.fori_loop(0, n_ch, idx_body, 0)

    def bit_body(bi, t):
        bit = lax.shift_left(jnp.int32(1), 31 - bi)
        cand = t ^ bit
        candb = jnp.broadcast_to(cand, (qb, LANES))

        def cnt_body(c, acc):
            k = keys_ref[c]
            return (acc + (k[:, :LANES] >= candb).astype(I32)
                    + (k[:, LANES:] >= candb).astype(I32))

        acc = lax.fori_loop(0, n_ch, cnt_body, jnp.zeros((qb, LANES), I32))
        cnt = jnp.sum(acc, axis=1, keepdims=True)
        return jnp.where(cnt >= k_top, cand, t)

    t = lax.fori_loop(0, 32, bit_body, jnp.full((qb, 1), INT_MIN, I32))
    tb_ref[...] = jnp.broadcast_to(jnp.maximum(t, jnp.int32(INT_MIN + 1)), (qb, KC))

    def head_body(h, carry):
        pair = h // 8

        def pass_a(c, m):
            kk = kvp_ref[pair, pl.ds(pl.multiple_of(c * KC, KC), KC), :]
            s = lax.dot_general(lhsq_ref[h], kk, nt, preferred_element_type=F32)
            s = s + tab_ref[2 * c - i + (n_qb - 1), pl.ds(h, 1), :]
            s = jnp.where(keys_ref[c] >= tb_ref[...], s, NEG_BIG)
            s_ref[c] = s
            return jnp.maximum(m, jnp.maximum(s[:, :LANES], s[:, LANES:]))

        m = lax.fori_loop(0, n_ch, pass_a, jnp.full((qb, LANES), NEG_BIG, F32))
        mrow = jnp.max(m, axis=1, keepdims=True)
        mb = jnp.broadcast_to(mrow, (qb, KC))

        def pass_b(c, lo):
            l, o = lo
            p = jnp.exp(s_ref[c] - mb)
            vv = kvp_ref[2 + pair, pl.ds(pl.multiple_of(c * KC, KC), KC), :]
            o = o + jnp.dot(p.astype(BF16), vv, preferred_element_type=F32)
            return l + p[:, :LANES] + p[:, LANES:], o

        l, o = lax.fori_loop(0, n_ch, pass_b,
                             (jnp.zeros((qb, LANES), F32), jnp.zeros((qb, LANES), F32)))
        oh_ref[h] = o / jnp.sum(l, axis=1, keepdims=True)
        return carry

    lax.fori_loop(0, N_HEADS, head_body, 0)

    for p in range(N_HEADS // 2):
        kh = (p // 2) % 2
        a, b = oh_ref[2 * p], oh_ref[2 * p + 1]
        if kh == 0:
            tile = jnp.where(low, a, pltpu.roll(b, HEAD_DIM, axis=1))
        else:
            tile = jnp.where(low, pltpu.roll(a, HEAD_DIM, axis=1), b)
        o_ref[:, LANES * p:LANES * (p + 1)] = tile.astype(o_ref.dtype)


def _alibi_table(n_qb):
    slopes = np.exp2(-8.0 * np.arange(1, N_HEADS + 1, dtype=np.float64) / N_HEADS).astype(np.float32)
    e = np.arange(-(n_qb - 1), 1, dtype=np.float64)[:, None, None]
    l = np.arange(KC, dtype=np.float64)[None, None, :]
    rel = (Q_BLOCK * e + l - (Q_BLOCK - 1)).astype(np.float32)
    return jnp.asarray(slopes[None, :, None] * rel)


def _prompt_attention(h, kvp, kd, b, s, col_q, col_qi, col_kiw):
    n_qb = s // Q_BLOCK
    k_top = min(TOPK_MAX, s // 4)
    n_kc = (s + KC - 1) // KC
    tab = _alibi_table(n_qb)
    kern = functools.partial(_attn_kernel, k_top=k_top, n_qb=n_qb)
    return pl.pallas_call(
        kern,
        grid=(b, n_qb),
        in_specs=[
            pl.BlockSpec((Q_BLOCK, ATT_W), lambda bi, i: (bi * n_qb + i, col_q // ATT_W)),
            pl.BlockSpec((Q_BLOCK, IDX_W), lambda bi, i: (bi * n_qb + i, col_qi // IDX_W)),
            pl.BlockSpec((Q_BLOCK, LANES), lambda bi, i: (bi * n_qb + i, col_kiw // LANES)),
            pl.BlockSpec((4, s, LANES), lambda bi, i: (0, bi, 0)),
            pl.BlockSpec((s, LANES), lambda bi, i: (bi, 0)),
            pl.BlockSpec((n_qb, N_HEADS, KC), lambda bi, i: (0, 0, 0)),
        ],
        out_specs=pl.BlockSpec((Q_BLOCK, ATT_W), lambda bi, i: (bi * n_qb + i, 0)),
        out_shape=jax.ShapeDtypeStruct((b * s, ATT_W), BF16),
        scratch_shapes=[
            pltpu.VMEM((N_HEADS, Q_BLOCK, LANES), BF16),
            pltpu.VMEM((IDX_HEADS, Q_BLOCK, LANES), BF16),
            pltpu.VMEM((IDX_HEADS, Q_BLOCK, LANES), F32),
            pltpu.VMEM((n_kc, Q_BLOCK, KC), I32),
            pltpu.VMEM((n_kc, Q_BLOCK, KC), F32),
            pltpu.VMEM((N_HEADS, Q_BLOCK, LANES), F32),
            pltpu.VMEM((Q_BLOCK, KC), I32),
        ],
        compiler_params=_cparams(("parallel", "arbitrary")),
        name="prompt_attn",
    )(h, h, h, kvp, kd, tab)


HALO = 16


def _pool_kernel(halo_ref, cur_ref, pw_ref, ps_ref, o_ref, *, tb, pos_base, zero_first):
    sb = pl.program_id(1)
    halo = halo_ref[...]
    if zero_first:
        halo = jnp.where(sb == 0, 0.0, halo)
    cur = cur_ref[...]
    gw = cur.shape[1] // len(POOL_WINDOWS)
    pos = pos_base + sb * tb + lax.broadcasted_iota(I32, (tb, 1), 0)
    for g, w in enumerate(POOL_WINDOWS):
        sl = slice(g * gw, (g + 1) * gw)
        acc = jnp.concatenate([halo[:, sl], cur[:, sl]], axis=0)
        step = 1
        while step < w:
            acc = acc + pltpu.roll(acc, step, axis=0)
            step *= 2
        cnt = jnp.minimum(w, pos + 1).astype(F32)
        diff = (acc[HALO:, :] / cnt - cur[:, sl]).astype(BF16)
        mixed = jnp.dot(diff, pw_ref[g], preferred_element_type=F32)
        o_ref[:, sl] = (mixed * ps_ref[:, sl]).astype(o_ref.dtype)


def _pool_prompt(h, pw16, ps, b, s, col_u, tb):
    c = pw16.shape[0] * pw16.shape[1]
    nsb = s // tb
    cb = col_u // c
    kern = functools.partial(_pool_kernel, tb=tb, pos_base=0, zero_first=True)
    return pl.pallas_call(
        kern,
        grid=(b, nsb),
        in_specs=[
            pl.BlockSpec((HALO, c), lambda bi, i: (jnp.maximum((bi * s + i * tb) // HALO - 1, 0), cb)),
            pl.BlockSpec((tb, c), lambda bi, i: (bi * nsb + i, cb)),
            pl.BlockSpec(pw16.shape, lambda bi, i: (0, 0, 0)),
            pl.BlockSpec((1, c), lambda bi, i: (0, 0)),
        ],
        out_specs=pl.BlockSpec((tb, c), lambda bi, i: (bi * nsb + i, 0)),
        out_shape=jax.ShapeDtypeStruct((b * s, c), BF16),
        compiler_params=_cparams(("parallel", "arbitrary")),
        name="pool_prompt",
    )(h, h, pw16, ps)


def _pool_sample(halo, us, pw16, ps, pos_base):
    db, ds, c = us.shape
    kern = functools.partial(_pool_kernel, tb=ds, pos_base=pos_base, zero_first=False)
    return pl.pallas_call(
        kern,
        grid=(db, 1),
        in_specs=[
            pl.BlockSpec((None, HALO, c), lambda bi, i: (bi, 0, 0)),
            pl.BlockSpec((None, ds, c), lambda bi, i: (bi, 0, 0)),
            pl.BlockSpec(pw16.shape, lambda bi, i: (0, 0, 0)),
            pl.BlockSpec((1, c), lambda bi, i: (0, 0)),
        ],
        out_specs=pl.BlockSpec((None, ds, c), lambda bi, i: (bi, 0, 0)),
        out_shape=jax.ShapeDtypeStruct((db, ds, c), BF16),
        compiler_params=_cparams(("parallel", "arbitrary")),
        name="pool_sample",
    )(halo, us, pw16, ps)


MERGE_TN = 512


def _merge_kernel(att_ref, pool_ref, woa_ref, wob_ref, ga_ref, gb_ref, o_ref):
    a = jnp.dot(att_ref[...], woa_ref[...], preferred_element_type=F32)
    b = jnp.dot(pool_ref[...], wob_ref[...], preferred_element_type=F32)
    m = jax.nn.sigmoid(ga_ref[...]) * a + jax.nn.sigmoid(gb_ref[...]) * b
    o_ref[...] = m.astype(o_ref.dtype)


def _merge(att, pool, woa16, wob16, h, col_ga, col_gb, tm):
    t = att.shape[0]
    d = woa16.shape[1]
    tn = MERGE_TN
    return pl.pallas_call(
        _merge_kernel,
        grid=(t // tm, d // tn),
        in_specs=[
            pl.BlockSpec((tm, att.shape[1]), lambda i, j: (i, 0)),
            pl.BlockSpec((tm, pool.shape[1]), lambda i, j: (i, 0)),
            pl.BlockSpec((woa16.shape[0], tn), lambda i, j: (0, j)),
            pl.BlockSpec((wob16.shape[0], tn), lambda i, j: (0, j)),
            pl.BlockSpec((tm, tn), lambda i, j: (i, col_ga // tn + j)),
            pl.BlockSpec((tm, tn), lambda i, j: (i, col_gb // tn + j)),
        ],
        out_specs=pl.BlockSpec((tm, tn), lambda i, j: (i, j)),
        out_shape=jax.ShapeDtypeStruct((t, d), BF16),
        compiler_params=_cparams(("parallel", "arbitrary")),
        name="merge",
    )(att, pool, woa16, wob16, h, h)


def _layer_norm(y, g, b):
    mu = jnp.mean(y, axis=-1, keepdims=True)
    yc = y - mu
    var = jnp.mean(yc * yc, axis=-1, keepdims=True)
    return yc * lax.rsqrt(var + LN_EPS) * g + b


def _post_kernel(x_ref, m_ref, wout_ref, g_ref, b_ref, wr_ref, br_ref, hp_ref, hpw_ref, lg_ref):
    mix = jnp.dot(m_ref[...], wout_ref[...], preferred_element_type=F32)
    hn = _layer_norm(ALPHA * x_ref[...] + mix, g_ref[...], b_ref[...])
    hp_ref[...] = hn
    h16 = hn.astype(BF16)
    lg_ref[...] = jnp.dot(h16, wr_ref[...], preferred_element_type=F32) + br_ref[...]
    half = hn.shape[1] // 2
    lo = pltpu.bitcast(h16[:, :half].astype(F32), U32)
    hi = pltpu.bitcast(h16[:, half:].astype(F32), U32)
    hpw_ref[...] = (hi & jnp.uint32(0xFFFF0000)) | (lo >> 16)


def _post(x2d, m, wout16, g, b, wr16, br, tm):
    t, d = x2d.shape
    return pl.pallas_call(
        _post_kernel,
        grid=(t // tm,),
        in_specs=[
            pl.BlockSpec((tm, d), lambda i: (i, 0)),
            pl.BlockSpec((tm, d), lambda i: (i, 0)),
            pl.BlockSpec((d, d), lambda i: (0, 0)),
            pl.BlockSpec((1, d), lambda i: (0, 0)),
            pl.BlockSpec((1, d), lambda i: (0, 0)),
            pl.BlockSpec((d, LANES), lambda i: (0, 0)),
            pl.BlockSpec((1, LANES), lambda i: (0, 0)),
        ],
        out_specs=[pl.BlockSpec((tm, d), lambda i: (i, 0)),
                   pl.BlockSpec((tm, d // 2), lambda i: (i, 0)),
                   pl.BlockSpec((tm, LANES), lambda i: (i, 0))],
        out_shape=[jax.ShapeDtypeStruct((t, d), F32),
                   jax.ShapeDtypeStruct((t, d // 2), U32),
                   jax.ShapeDtypeStruct((t, LANES), F32)],
        compiler_params=_cparams(("parallel",)),
        name="post_ln1",
    )(x2d, m, wout16, g, b, wr16, br)


MOE_TM = 256


def _top4(l):
    tm = l.shape[0]
    lane = lax.broadcasted_iota(I32, (tm, LANES), 1)
    vals, idxs = [], []
    for _ in range(TOP_K):
        mk = jnp.max(l, axis=1, keepdims=True)
        ek = jnp.min(jnp.where(l == mk, lane, LANES), axis=1, keepdims=True)
        vals.append(mk)
        idxs.append(ek)
        l = jnp.where(lane == ek, -jnp.inf, l)
    onehot = jnp.zeros((tm, LANES), F32)
    for ek in idxs:
        onehot = onehot + (lane == ek).astype(F32)
    return vals, idxs, onehot, lane


def _route_count_kernel(lg_ref, sizes_ref):
    _, _, onehot, _ = _top4(lg_ref[...])

    @pl.when(pl.program_id(0) == 0)
    def _():
        sizes_ref[...] = jnp.zeros_like(sizes_ref)

    sizes_ref[...] = sizes_ref[...] + jnp.sum(onehot, axis=0, keepdims=True)


def _route_kernel(lg_ref, sizes_ref, gate_ref, dest_ref, meta_ref, exm_ref, carry_ref, start_ref,
                  *, tm, n_exp, nb_pad):
    i = pl.program_id(0)
    vals, idxs, onehot, lane = _top4(lg_ref[...])
    ex = [jnp.exp(v - vals[0]) for v in vals]
    den = ex[0] + ex[1] + ex[2] + ex[3]

    @pl.when(i == 0)
    def _():
        sizes = sizes_ref[...]
        padded = jnp.floor((sizes + (MOE_TM - 1)) * (1.0 / MOE_TM)) * MOE_TM
        lane8 = lax.broadcasted_iota(I32, (SUBLANES, LANES), 1)
        ends = jnp.broadcast_to(padded, (SUBLANES, LANES))
        sh = 1
        while sh < LANES:
            ends = ends + jnp.where(lane8 >= sh, pltpu.roll(ends, sh, axis=1), 0.0)
            sh *= 2
        start = ends - jnp.broadcast_to(padded, (SUBLANES, LANES))
        start_ref[...] = start[0:1]
        carry_ref[...] = jnp.zeros_like(carry_ref)
        row8 = lax.broadcasted_iota(I32, (SUBLANES, LANES), 0)
        total = jnp.max(ends[0:1], axis=1, keepdims=True) * (1.0 / MOE_TM)
        exm = jnp.where(row8 == 0, start,
                        jnp.where(row8 == 1, jnp.broadcast_to(sizes, (SUBLANES, LANES)),
                                  jnp.where(row8 == 2, jnp.broadcast_to(padded, (SUBLANES, LANES)),
                                            jnp.broadcast_to(total, (SUBLANES, LANES)))))
        exm_ref[...] = exm.astype(I32)
        rb = lax.broadcasted_iota(I32, (nb_pad, LANES), 0).astype(F32) * MOE_TM
        lanen = lax.broadcasted_iota(I32, (nb_pad, LANES), 1)
        done = jnp.where(jnp.logical_and(jnp.broadcast_to(ends[0:1], (nb_pad, LANES)) <= rb, lanen < n_exp), 1.0, 0.0)
        be = jnp.minimum(jnp.sum(done, axis=1, keepdims=True), n_exp - 1.0)
        meta = jnp.where(lanen == 0, jnp.broadcast_to(be, (nb_pad, LANES)),
                         jnp.broadcast_to(total, (nb_pad, LANES)))
        meta_ref[...] = meta.astype(I32)

    r = lax.broadcasted_iota(I32, (tm, tm), 0)
    c = lax.broadcasted_iota(I32, (tm, tm), 1)
    tri = jnp.where(c < r, 1.0, 0.0).astype(BF16)
    cum = jnp.dot(tri, onehot.astype(BF16), preferred_element_type=F32)
    base = cum + carry_ref[...] + start_ref[...]
    carry_ref[...] = carry_ref[...] + jnp.sum(onehot, axis=0, keepdims=True)

    g_out = jnp.zeros((tm, LANES), F32)
    d_out = jnp.zeros((tm, LANES), F32)
    for k in range(TOP_K):
        dk = jnp.sum(jnp.where(lane == idxs[k], base, 0.0), axis=1, keepdims=True)
        g_out = jnp.where(lane == k, ex[k] / den, g_out)
        d_out = jnp.where(lane == k, dk, d_out)
    gate_ref[...] = g_out
    dest_ref[...] = d_out.astype(I32)


def _route(logits, n_exp, tm, nb_pad):
    t = logits.shape[0]
    tok = pl.BlockSpec((tm, LANES), lambda i: (i, 0))
    one = pl.BlockSpec((1, LANES), lambda i: (0, 0))
    sizes = pl.pallas_call(
        _route_count_kernel,
        grid=(t // tm,),
        in_specs=[tok],
        out_specs=one,
        out_shape=jax.ShapeDtypeStruct((1, LANES), F32),
        compiler_params=_cparams(("arbitrary",)),
        name="route_count",
    )(logits)
    kern = functools.partial(_route_kernel, tm=tm, n_exp=n_exp, nb_pad=nb_pad)
    return pl.pallas_call(
        kern,
        grid=(t // tm,),
        in_specs=[tok, one],
        out_specs=[tok, tok,
                   pl.BlockSpec((nb_pad, LANES), lambda i: (0, 0)),
                   pl.BlockSpec((SUBLANES, LANES), lambda i: (0, 0))],
        out_shape=[jax.ShapeDtypeStruct((t, LANES), F32),
                   jax.ShapeDtypeStruct((t, LANES), I32),
                   jax.ShapeDtypeStruct((nb_pad, LANES), I32),
                   jax.ShapeDtypeStruct((SUBLANES, LANES), I32)],
        scratch_shapes=[pltpu.VMEM((1, LANES), F32), pltpu.VMEM((1, LANES), F32)],
        compiler_params=_cparams(("arbitrary",)),
        name="route",
    )(logits, sizes)


def _dispatch_kernel(exm_ref, dest_p_ref, dest_s_ref, hw_p_ref, hw_s_ref, xs_ref, zblk_ref, sem, zsem,
                     *, tm, ts, n_exp, nb):
    def row_copy(src_ref, r, d):
        return pltpu.make_async_copy(src_ref.at[pl.ds(r, 1)], xs_ref.at[pl.ds(d, 1)], sem)

    def block_zero(row0):
        return pltpu.make_async_copy(zblk_ref, xs_ref.at[pl.ds(pl.multiple_of(row0, MOE_TM), MOE_TM)], zsem)

    def scatter_rows(src_ref, dest_ref, n):
        def issue(r, carry):
            for k in range(TOP_K):
                row_copy(src_ref, r, dest_ref[r * TOP_K + k]).start()
            return carry

        def drain(r, carry):
            for k in range(TOP_K):
                row_copy(src_ref, r, dest_ref[r * TOP_K + k]).wait()
            return carry

        lax.fori_loop(0, n, issue, 0)
        lax.fori_loop(0, n, drain, 0)

    @pl.when(pl.program_id(0) == 0)
    def _():
        zblk_ref[...] = jnp.zeros_like(zblk_ref)
        n_valid = exm_ref[3 * LANES]

        def zero_blocks(wait):
            def per_expert(e, carry):
                padded = exm_ref[2 * LANES + e]

                @pl.when(padded > 0)
                def _():
                    cp = block_zero(exm_ref[e] + padded - MOE_TM)
                    cp.wait() if wait else cp.start()

                return carry

            def per_tail(rb, carry):
                cp = block_zero(rb * MOE_TM)
                cp.wait() if wait else cp.start()
                return carry

            lax.fori_loop(0, n_exp, per_expert, 0)
            lax.fori_loop(n_valid, nb, per_tail, 0)

        zero_blocks(False)
        zero_blocks(True)
        scatter_rows(hw_s_ref, dest_s_ref, ts)

    scatter_rows(hw_p_ref, dest_p_ref, tm)


def _dispatch(exm_flat, dest_p, dest_s, hw_p, hw_s, nb, tm, n_exp):
    tp, w = hw_p.shape
    ts = hw_s.shape[0]
    kern = functools.partial(_dispatch_kernel, tm=tm, ts=ts, n_exp=n_exp, nb=nb)
    return pl.pallas_call(
        kern,
        grid_spec=pltpu.PrefetchScalarGridSpec(
            num_scalar_prefetch=1,
            grid=(tp // tm,),
            in_specs=[pl.BlockSpec((tm * TOP_K,), lambda i, exm: (i,), memory_space=pltpu.SMEM),
                      pl.BlockSpec((ts * TOP_K,), lambda i, exm: (0,), memory_space=pltpu.SMEM),
                      pl.BlockSpec((tm, w), lambda i, exm: (i, 0)),
                      pl.BlockSpec((ts, w), lambda i, exm: (0, 0))],
            out_specs=pl.BlockSpec(memory_space=pl.ANY),
            scratch_shapes=[pltpu.VMEM((MOE_TM, w), U32), pltpu.SemaphoreType.DMA(()),
                            pltpu.SemaphoreType.DMA(())],
        ),
        out_shape=jax.ShapeDtypeStruct((nb * MOE_TM, w), U32),
        compiler_params=_cparams(("arbitrary",)),
        name="dispatch",
    )(exm_flat, dest_p, dest_s, hw_p, hw_s)


MOE_TF = 512
MOE_TN = 512


def _weights_changed(be_ref, rb):
    prev = be_ref[jnp.maximum(rb - 1, 0)]
    return jnp.logical_or(rb == 0, be_ref[rb] != prev)


def _moe_a_kernel(be_ref, nv_ref, xs_ref, wg_ref, wu_ref, bg_ref, bu_ref, act_ref, wg16_ref, wu16_ref):
    rb = pl.program_id(1)

    @pl.when(_weights_changed(be_ref, rb))
    def _():
        wg16_ref[...] = wg_ref[...].astype(BF16)
        wu16_ref[...] = wu_ref[...].astype(BF16)

    @pl.when(rb < nv_ref[0])
    def _():
        xw = xs_ref[...]
        lo = pltpu.bitcast(xw << 16, F32)
        hi = pltpu.bitcast(xw & jnp.uint32(0xFFFF0000), F32)
        x16 = jnp.concatenate([lo, hi], axis=1).astype(BF16)
        hg = jnp.dot(x16, wg16_ref[...], preferred_element_type=F32) + bg_ref[...]
        hu = jnp.dot(x16, wu16_ref[...], preferred_element_type=F32) + bu_ref[...]
        g = jnp.minimum(hg, SWIGLU_LIMIT)
        up = jnp.clip(hu, -SWIGLU_LIMIT, SWIGLU_LIMIT)
        act_ref[...] = ((up + 1.0) * (g * jax.nn.sigmoid(SWIGLU_ALPHA * g))).astype(act_ref.dtype)

    @pl.when(rb >= nv_ref[0])
    def _():
        act_ref[...] = jnp.zeros_like(act_ref)


def _moe_b_kernel(be_ref, nv_ref, act_ref, wd_ref, bd_ref, y_ref, wd16_ref):
    rb = pl.program_id(1)

    @pl.when(_weights_changed(be_ref, rb))
    def _():
        wd16_ref[...] = wd_ref[...].astype(BF16)

    y_ref[...] = jnp.dot(act_ref[...], wd16_ref[...], preferred_element_type=F32) + bd_ref[...]


def _moe(block_e, n_valid, xs, w_gu, b_gu3, w_down, b_down3, nb):
    n_exp, d, f2 = w_gu.shape
    f = f2 // 2
    tf, tn = MOE_TF, MOE_TN
    nfj = f // tf

    def row_a(j, rb, be, nv):
        return (jnp.minimum(rb, nv[0] - 1), 0)

    act = pl.pallas_call(
        _moe_a_kernel,
        grid_spec=pltpu.PrefetchScalarGridSpec(
            num_scalar_prefetch=2,
            grid=(nfj, nb),
            in_specs=[
                pl.BlockSpec((MOE_TM, d // 2), row_a),
                pl.BlockSpec((None, d, tf), lambda j, rb, be, nv: (be[rb], 0, j)),
                pl.BlockSpec((None, d, tf), lambda j, rb, be, nv: (be[rb], 0, nfj + j)),
                pl.BlockSpec((None, 1, tf), lambda j, rb, be, nv: (be[rb], 0, j)),
                pl.BlockSpec((None, 1, tf), lambda j, rb, be, nv: (be[rb], 0, nfj + j)),
            ],
            out_specs=pl.BlockSpec((MOE_TM, tf), lambda j, rb, be, nv: (rb, j)),
            scratch_shapes=[pltpu.VMEM((d, tf), BF16), pltpu.VMEM((d, tf), BF16)],
        ),
        out_shape=jax.ShapeDtypeStruct((nb * MOE_TM, f), BF16),
        compiler_params=_cparams(("arbitrary", "arbitrary")),
        name="moe_gate_up",
    )(block_e, n_valid, xs, w_gu, w_gu, b_gu3, b_gu3)

    y = pl.pallas_call(
        _moe_b_kernel,
        grid_spec=pltpu.PrefetchScalarGridSpec(
            num_scalar_prefetch=2,
            grid=(d // tn, nb),
            in_specs=[
                pl.BlockSpec((MOE_TM, f), lambda j, rb, be, nv: (rb, 0)),
                pl.BlockSpec((None, f, tn), lambda j, rb, be, nv: (be[rb], 0, j)),
                pl.BlockSpec((None, 1, tn), lambda j, rb, be, nv: (be[rb], 0, j)),
            ],
            out_specs=pl.BlockSpec((MOE_TM, tn), lambda j, rb, be, nv: (rb, j)),
            scratch_shapes=[pltpu.VMEM((f, tn), BF16)],
        ),
        out_shape=jax.ShapeDtypeStruct((nb * MOE_TM, d), F32),
        compiler_params=_cparams(("arbitrary", "arbitrary")),
        name="moe_down",
    )(block_e, n_valid, act, w_down, b_down3)
    return y


def _combine_kernel(dest_ref, hp_ref, gate_ref, g_ref, b_ref, ys_ref, o_ref, gath_ref, sem, *, tm):
    def row_copy(r, k):
        return pltpu.make_async_copy(ys_ref.at[pl.ds(dest_ref[r * TOP_K + k], 1)],
                                     gath_ref.at[k, pl.ds(r, 1)], sem)

    def issue(r, carry):
        for k in range(TOP_K):
            row_copy(r, k).start()
        return carry

    lax.fori_loop(0, tm, issue, 0)

    def drain(r, carry):
        for k in range(TOP_K):
            row_copy(r, k).wait()
        return carry

    lax.fori_loop(0, tm, drain, 0)

    gates = gate_ref[...]
    f = gath_ref[0] * gates[:, 0:1]
    for k in range(1, TOP_K):
        f = f + gath_ref[k] * gates[:, k:k + 1]
    o_ref[...] = _layer_norm(ALPHA * hp_ref[...] + f, g_ref[...], b_ref[...])


def _combine(dest_flat, hp, gates, g, b, ys, tm):
    t, d = hp.shape
    kern = functools.partial(_combine_kernel, tm=tm)
    return pl.pallas_call(
        kern,
        grid=(t // tm,),
        in_specs=[
            pl.BlockSpec((tm * TOP_K,), lambda i: (i,), memory_space=pltpu.SMEM),
            pl.BlockSpec((tm, d), lambda i: (i, 0)),
            pl.BlockSpec((tm, LANES), lambda i: (i, 0)),
            pl.BlockSpec((1, d), lambda i: (0, 0)),
            pl.BlockSpec((1, d), lambda i: (0, 0)),
            pl.BlockSpec(memory_space=pl.ANY),
        ],
        out_specs=pl.BlockSpec((tm, d), lambda i: (i, 0)),
        out_shape=jax.ShapeDtypeStruct((t, d), F32),
        scratch_shapes=[pltpu.VMEM((TOP_K, tm, d), F32), pltpu.SemaphoreType.DMA(())],
        compiler_params=_cparams(("arbitrary",)),
        name="combine_ln2",
    )(dest_flat, hp, gates, g, b, ys)


PG = 8


def _sample_attn_kernel(pt_ref, lhsi_ref, wcol_ref, lhsq_ref, slope_ref, kin_ref, kn_ref, vn_ref, *rest,
                        n_pg, page, ds, k_top):
    kidx_refs = rest[:PG]
    k_refs = rest[PG:2 * PG]
    v_refs = rest[2 * PG:3 * PG]
    o_ref, keys_ref, t_ref, m_ref, l_ref, acc_ref = rest[3 * PG:]
    ph = pl.program_id(1)
    pg = pl.program_id(2)
    n_pages = n_pg * PG
    rows = IDX_HEADS * ds
    nt = (((1,), (1,)), ((), ()))
    lane = lax.broadcasted_iota(I32, (ds, page), 1)
    qrow = lax.broadcasted_iota(I32, (ds, page), 0)

    def scores(kid16):
        d = lax.dot_general(lhsi_ref[...], kid16, nt, preferred_element_type=F32)
        r = jnp.maximum(d, 0.0) * wcol_ref[...]
        acc = r[0:ds]
        for h in range(1, IDX_HEADS):
            acc = acc + r[h * ds:(h + 1) * ds]
        bits = pltpu.bitcast(acc, I32)
        return jnp.where(bits < 0, bits ^ jnp.int32(0x7FFFFFFF), bits)

    @pl.when(ph == 0)
    def _():
        for g in range(PG):
            keys_ref[pg * PG + g] = scores(kidx_refs[g][...].astype(BF16))

        @pl.when(pg == n_pg - 1)
        def _():
            key = scores(kin_ref[...].astype(BF16))
            keys_ref[n_pages] = jnp.where(lane <= qrow, key, jnp.int32(INT_MIN))

            def bit_body(bi, t):
                bit = lax.shift_left(jnp.int32(1), 31 - bi)
                cand = t ^ bit
                candb = jnp.broadcast_to(cand, (ds, page))

                def cnt_body(c, acc):
                    return acc + (keys_ref[c] >= candb).astype(I32)

                acc = lax.fori_loop(0, n_pages + 1, cnt_body, jnp.zeros((ds, page), I32))
                cnt = jnp.sum(acc, axis=1, keepdims=True)
                return jnp.where(cnt >= k_top, cand, t)

            t = lax.fori_loop(0, 32, bit_body, jnp.full((ds, 1), INT_MIN, I32))
            t_ref[...] = jnp.broadcast_to(jnp.maximum(t, jnp.int32(INT_MIN + 1)), (ds, page))
            m_ref[...] = jnp.full(m_ref.shape, NEG_BIG, F32)
            l_ref[...] = jnp.zeros_like(l_ref)
            acc_ref[...] = jnp.zeros_like(acc_ref)

    def attend(kpage, vpage, key_idx, kpos0):
        sel8 = keys_ref[key_idx] >= t_ref[...]
        sel = jnp.concatenate([sel8.astype(I32)] * IDX_HEADS, axis=0) > 0
        s = lax.dot_general(lhsq_ref[...], kpage.astype(BF16), nt, preferred_element_type=F32)
        rowi = lax.broadcasted_iota(I32, (rows, page), 0)
        lanei = lax.broadcasted_iota(I32, (rows, page), 1)
        qpos = n_pages * page + (rowi & (ds - 1))
        dist = (qpos - (kpos0 + lanei)).astype(F32)
        s = jnp.where(sel, s - slope_ref[...] * dist, NEG_BIG)
        m_old = m_ref[...]
        m_new = jnp.maximum(m_old, jnp.max(s, axis=1, keepdims=True))
        alpha = jnp.exp(m_old - m_new)
        p = jnp.where(sel, jnp.exp(s - m_new), 0.0)
        l_ref[...] = alpha * l_ref[...] + jnp.sum(p, axis=1, keepdims=True)
        acc_ref[...] = alpha * acc_ref[...] + jnp.dot(p.astype(BF16), vpage.astype(BF16),
                                                      preferred_element_type=F32)
        m_ref[...] = m_new

    @pl.when(ph == 1)
    def _():
        for g in range(PG):
            pidx = pg * PG + g
            attend(k_refs[g][...], v_refs[g][...], pidx, pidx * page)

        @pl.when(pg == n_pg - 1)
        def _():
            attend(kn_ref[...], vn_ref[...], n_pages, n_pages * page)
            o_ref[...] = acc_ref[...] / l_ref[...]


def _sample_attention(page_table, lhsi, wcol, lhsq, slope_col, ki_new, k_new, v_new, ck, cv, cki):
    db, n_pages = page_table.shape
    n_phys, page, kvw = ck.shape
    ds = lhsi.shape[1] // IDX_HEADS
    n_pg = n_pages // PG
    rows = lhsi.shape[1]
    k_top = min(TOPK_MAX, (n_pages * page + ds) // 4)
    kern = functools.partial(_sample_attn_kernel, n_pg=n_pg, page=page, ds=ds, k_top=k_top)

    def kidx_map(g):
        return lambda b, ph, pg, pt: (pt[b, (pg * (1 - ph) + (n_pg - 1) * ph) * PG + g], 0, 0)

    def kv_map(g):
        return lambda b, ph, pg, pt: (pt[b, pg * ph * PG + g], 0, 0)

    per_b = lambda b, ph, pg, pt: (b, 0, 0)
    in_specs = [
        pl.BlockSpec((None, rows, IDX_DIM), per_b),
        pl.BlockSpec((None, rows, page), per_b),
        pl.BlockSpec((None, rows, kvw), per_b),
        pl.BlockSpec((rows, page), lambda b, ph, pg, pt: (0, 0)),
        pl.BlockSpec((None, page, IDX_DIM), per_b),
        pl.BlockSpec((None, page, kvw), per_b),
        pl.BlockSpec((None, page, kvw), per_b),
    ]
    in_specs += [pl.BlockSpec((None, page, IDX_DIM), kidx_map(g)) for g in range(PG)]
    in_specs += [pl.BlockSpec((None, page, kvw), kv_map(g)) for g in range(PG)]
    in_specs += [pl.BlockSpec((None, page, kvw), kv_map(g)) for g in range(PG)]
    return pl.pallas_call(
        kern,
        grid_spec=pltpu.PrefetchScalarGridSpec(
            num_scalar_prefetch=1,
            grid=(db, 2, n_pg),
            in_specs=in_specs,
            out_specs=pl.BlockSpec((None, rows, kvw), per_b),
            scratch_shapes=[
                pltpu.VMEM((n_pages + 1, ds, page), I32),
                pltpu.VMEM((ds, page), I32),
                pltpu.VMEM((rows, 1), F32),
                pltpu.VMEM((rows, 1), F32),
                pltpu.VMEM((rows, kvw), F32),
            ],
        ),
        out_shape=jax.ShapeDtypeStruct((db, rows, kvw), F32),
        compiler_params=_cparams(("arbitrary", "arbitrary", "arbitrary")),
        name="sample_attn",
    )(page_table, lhsi, wcol, lhsq, slope_col, ki_new, k_new, v_new,
      *([cki] * PG), *([ck] * PG), *([cv] * PG))


def _round_up(x, m):
    return (x + m - 1) // m * m


def _row_tile(t, cap):
    tm = min(cap, t)
    while t % tm:
        tm //= 2
    return tm


def kernel(x_prompt, x_sample, cache_k, cache_v, cache_kidx, state_pool, page_table, w_in, w_oa, w_ob, pool_w, pool_scale, w_out, ln1_g, ln1_b, w_router, b_router, w_gu, b_gu, w_down, b_down, ln2_g, ln2_b):
    b, s, d = x_prompt.shape
    db, ds, _ = x_sample.shape
    depth = w_in.shape[0]
    assert depth == DEPTH == 1
    pool_c = d // 2
    n_exp = w_router.shape[-1]
    n_phys, page = cache_k.shape[1], cache_k.shape[2]
    n_pages = page_table.shape[1]
    past = n_pages * page
    assert ds * IDX_HEADS == LANES and page == LANES and n_pages % PG == 0 and n_exp <= LANES
    assert s % (2 * Q_BLOCK) == 0

    w = w_in[0]
    o_q, o_k, o_v, o_qi = 0, ATT_W, ATT_W + KV_W, ATT_W + 2 * KV_W
    o_ki = o_qi + IDX_W
    o_wi = o_ki + IDX_DIM
    o_u = o_wi + IDX_HEADS
    o_ga = o_u + pool_c
    o_gb = o_ga + d
    col_q, col_qi, col_u = 0, ATT_W, ATT_W + IDX_W
    col_ga = col_u + pool_c
    col_gb = col_ga + d
    col_kv = col_gb + d
    col_kiw = col_kv + 2 * KV_W
    n_cols = _round_up(col_kiw + LANES, IN_TN)
    assert col_ga % MERGE_TN == 0 and col_gb % MERGE_TN == 0 and col_kv % (2 * KV_W) == 0 and col_u % pool_c == 0
    w_p = jnp.concatenate([
        w[:, o_q:o_q + ATT_W], w[:, o_qi:o_qi + IDX_W], w[:, o_u:o_u + pool_c],
        w[:, o_ga:o_ga + d], w[:, o_gb:o_gb + d], w[:, o_k:o_k + KV_W], w[:, o_v:o_v + KV_W],
        w[:, o_ki:o_ki + IDX_DIM], w[:, o_wi:o_wi + IDX_HEADS],
        jnp.zeros((d, n_cols - col_kiw - IDX_DIM - IDX_HEADS), w.dtype)], axis=1).astype(BF16)
    woa16 = w_oa[0].astype(BF16)
    wob16 = w_ob[0].astype(BF16)
    wout16 = w_out[0].astype(BF16)
    pw16 = pool_w[0].astype(BF16)
    ps = pool_scale[0].reshape(1, pool_c)
    wr16 = jnp.pad(w_router[0], ((0, 0), (0, LANES - n_exp))).astype(BF16)
    br = jnp.pad(b_router[0].astype(F32), (0, LANES - n_exp), constant_values=NEG_BIG).reshape(1, LANES)
    g1, b1 = ln1_g[0].reshape(1, d), ln1_b[0].reshape(1, d)
    g2, b2 = ln2_g[0].reshape(1, d), ln2_b[0].reshape(1, d)

    tp = b * s
    ts = db * ds
    xp2 = x_prompt.reshape(tp, d)
    xs2 = x_sample.reshape(ts, d)

    hp_in = _in_proj(xp2, w_p, _row_tile(tp, 512))
    kvp, kd = _kv_cast(hp_in, col_kv, col_kiw, _row_tile(tp, 1024))
    att_p = _prompt_attention(hp_in, kvp, kd, b, s, col_q, col_qi, col_kiw)
    pool_p = _pool_prompt(hp_in, pw16, ps, b, s, col_u, _row_tile(s, 512))
    m_p = _merge(att_p, pool_p, woa16, wob16, hp_in, col_ga, col_gb, _row_tile(tp, 512))
    h1_p, hw_p, lg_p = _post(xp2, m_p, wout16, g1, b1, wr16, br, _row_tile(tp, 256))

    hs_in = _in_proj(xs2, w_p, ts)
    q_s = hs_in[:, col_q:col_q + ATT_W].reshape(db, ds, N_HEADS, HEAD_DIM)
    qi_s = hs_in[:, col_qi:col_qi + IDX_W].reshape(db, ds, IDX_HEADS, IDX_DIM)
    k_s = hs_in[:, col_kv:col_kv + KV_W].reshape(db, ds, KV_W)
    v_s = hs_in[:, col_kv + KV_W:col_kv + 2 * KV_W].reshape(db, ds, KV_W)
    ki_s = hs_in[:, col_kiw:col_kiw + IDX_DIM].reshape(db, ds, IDX_DIM)
    wi_s = hs_in[:, col_kiw + IDX_DIM:col_kiw + IDX_DIM + IDX_HEADS].reshape(db, ds, IDX_HEADS)
    u_s = hs_in[:, col_u:col_u + pool_c].reshape(db, ds, pool_c)
    rows = IDX_HEADS * ds
    lhsi = jnp.transpose(qi_s, (0, 2, 1, 3)).reshape(db, rows, IDX_DIM).astype(BF16)
    wscale = (IDX_DIM ** -0.5) * (IDX_HEADS ** -0.5)
    wcol = jnp.broadcast_to((jnp.transpose(wi_s, (0, 2, 1)).reshape(db, rows, 1) * wscale), (db, rows, page))
    qh = jnp.transpose(q_s, (0, 2, 1, 3)) * (HEAD_DIM ** -0.5)
    grp = (jnp.arange(N_HEADS) // (N_HEADS // N_KV_HEADS))
    onehot_g = (grp[:, None] == jnp.arange(N_KV_HEADS)[None, :]).astype(F32)
    lhsq = (qh[:, :, :, None, :] * onehot_g[None, :, None, :, None]).reshape(db, rows, KV_W).astype(BF16)
    slopes = jnp.exp2(-8.0 * jnp.arange(1, N_HEADS + 1, dtype=F32) / N_HEADS)
    slope_col = jnp.broadcast_to(jnp.repeat(slopes, ds).reshape(rows, 1), (rows, page))
    padr = ((0, 0), (0, page - ds), (0, 0))
    o_s = _sample_attention(page_table, lhsi, wcol, lhsq, slope_col,
                            jnp.pad(ki_s, padr), jnp.pad(k_s, padr), jnp.pad(v_s, padr),
                            cache_k[0].reshape(n_phys, page, KV_W), cache_v[0].reshape(n_phys, page, KV_W),
                            cache_kidx[0])
    o_s = o_s.reshape(db, N_HEADS, ds, N_KV_HEADS, HEAD_DIM)
    att_s = jnp.stack([o_s[:, h, :, h // (N_HEADS // N_KV_HEADS), :] for h in range(N_HEADS)], axis=2)
    att_s = att_s.reshape(ts, ATT_W).astype(BF16)
    halo_s = jnp.concatenate([jnp.zeros((db, HALO - POOL_STATE, pool_c), F32), state_pool[0]], axis=1)
    pool_s = _pool_sample(halo_s, u_s, pw16, ps, past).reshape(ts, pool_c)
    m_s = _merge(att_s, pool_s, woa16, wob16, hs_in, col_ga, col_gb, ts)
    h1_s, hw_s, lg_s = _post(xs2, m_s, wout16, g1, b1, wr16, br, ts)

    t_all = tp + ts
    logits = jnp.concatenate([lg_p, lg_s], axis=0)
    rt = SUBLANES
    for cand in range(SUBLANES, 1025, SUBLANES):
        if t_all % cand == 0:
            rt = cand
    nb = (t_all * TOP_K + MOE_TM - 1) // MOE_TM + n_exp
    nb_pad = _round_up(nb, SUBLANES)
    gates, dest, meta, exm = _route(logits, n_exp, rt, nb_pad)
    block_e = meta[:, 0]
    n_valid = meta[0:1, 1]
    dest_flat = dest[:, :TOP_K].reshape(t_all * TOP_K)
    exm_flat = exm[:4].reshape(4 * LANES)
    tdp = _row_tile(tp, 256)
    xs = _dispatch(exm_flat, dest_flat[:tp * TOP_K], dest_flat[tp * TOP_K:], hw_p, hw_s, nb, tdp, n_exp)
    ys = _moe(block_e, n_valid, xs, w_gu[0], b_gu[0].reshape(n_exp, 1, -1), w_down[0],
              b_down[0].reshape(n_exp, 1, -1), nb)
    y_p = _combine(dest_flat[:tp * TOP_K], h1_p, gates[:tp], g2, b2, ys, tdp)
    y_s = _combine(dest_flat[tp * TOP_K:], h1_s, gates[tp:], g2, b2, ys, ts)

    k_p = hp_in[:, col_kv:col_kv + KV_W].reshape(1, b, s, N_KV_HEADS, HEAD_DIM)
    v_p = hp_in[:, col_kv + KV_W:col_kv + 2 * KV_W].reshape(1, b, s, N_KV_HEADS, HEAD_DIM)
    ki_p = hp_in[:, col_kiw:col_kiw + IDX_DIM].reshape(1, b, s, IDX_DIM)
    u_p = hp_in[:, col_u:col_u + pool_c].reshape(b, s, pool_c)
    pst_p = u_p[:, s - POOL_STATE:, :][None]
    pst_s = jnp.concatenate([state_pool[0], u_s], axis=1)[:, -POOL_STATE:, :][None]
    return (y_p.reshape(b, s, d), y_s.reshape(db, ds, d), k_p, v_p, ki_p, pst_p,
            k_s.reshape(1, db, ds, N_KV_HEADS, HEAD_DIM), v_s.reshape(1, db, ds, N_KV_HEADS, HEAD_DIM),
            ki_s.reshape(1, db, ds, IDX_DIM), pst_s)
```
